```python
import jax, jax.numpy as jnp
from jax import lax
import numpy as np

D_MODEL = 1024
BATCH = 8
SEQ = 2048
DEPTH = 4

N_MIXERS = 2
EXPAND = 2
D_INNER = EXPAND * D_MODEL
EPS = 1e-6

SGU_CHUNK = 128
SGU_GROUPS = 8
SGU_GROUP_DIM = D_INNER // SGU_GROUPS
A_IN = 3 * D_INNER

GLA_HEADS = 4
GLA_DK_TOTAL = D_MODEL // 2
GLA_DK = GLA_DK_TOTAL // GLA_HEADS
GLA_DV = D_INNER // GLA_HEADS
GLA_GATE_RANK = 16
GLA_GATE_TAU = 16.0
GLA_CHUNK = 64
B_SPLITS = [GLA_DK_TOTAL, 2 * GLA_DK_TOTAL, 2 * GLA_DK_TOTAL + D_INNER, 2 * GLA_DK_TOTAL + 2 * D_INNER]
B_IN = 2 * GLA_DK_TOTAL + 2 * D_INNER + GLA_GATE_RANK

N_A = (DEPTH + 1) // 2
N_B = DEPTH // 2

kernel_name = "hybrid_sgu_gla_adaln_trunk"


def rmsnorm(x, g):
    xf = x.astype(jnp.float32)
    y = xf * lax.rsqrt(jnp.mean(xf * xf, axis=-1, keepdims=True) + EPS)
    return (y * g.astype(jnp.float32)).astype(x.dtype)


def sgu_branch(h, w_in, w_s, b_s, g_v, w_out):
    bsz, seq, _ = h.shape
    u, v, z = jnp.split(h @ w_in, 3, axis=-1)
    u = jax.nn.gelu(u)
    v = rmsnorm(jax.nn.gelu(v), g_v)
    n_c = seq // SGU_CHUNK
    v = v.reshape(bsz, n_c, SGU_CHUNK, SGU_GROUPS, SGU_GROUP_DIM)
    causal = jnp.tril(jnp.ones((SGU_CHUNK, SGU_CHUNK), dtype=bool))
    w = jnp.where(causal[None], w_s, 0)
    mixed = jnp.einsum('gts,bcsgd->bctgd', w, v) + b_s.T[None, None, :, :, None]
    y = u * mixed.reshape(bsz, seq, D_INNER) * jax.nn.silu(z)
    return y @ w_out


def gla_branch(h, w_in, w_gate_up, b_gate, g_o, w_out):
    f32 = jnp.float32
    bsz, seq, _ = h.shape
    q, k, v, z, a_lr = jnp.split(h @ w_in, B_SPLITS, axis=-1)
    log_a = jax.nn.log_sigmoid((a_lr @ w_gate_up + b_gate).astype(f32)) / GLA_GATE_TAU
    L = GLA_CHUNK
    n_c = seq // L

    def heads(t, d):
        return t.astype(f32).reshape(bsz, n_c, L, GLA_HEADS, d).transpose(0, 1, 3, 2, 4)

    q = heads(q, GLA_DK) * (GLA_DK ** -0.5)
    k = heads(k, GLA_DK)
    v = heads(v, GLA_DV)
    b = jnp.cumsum(heads(log_a, GLA_DK), axis=3)
    b_last = b[:, :, :, -1:, :]
    q_dec = q * jnp.exp(b)
    k_inv = k * jnp.exp(-b)
    k_state = k * jnp.exp(b_last - b)
    causal = jnp.tril(jnp.ones((L, L), dtype=bool))
    attn = jnp.where(causal, jnp.einsum('bchtk,bchsk->bchts', q_dec, k_inv), 0.0)
    o_intra = jnp.einsum('bchts,bchsv->bchtv', attn, v)
    decay = jnp.exp(b_last[:, :, :, 0, :])

    def step(state, inp):
        q_c, ks_c, v_c, d_c = inp
        o = jnp.einsum('bhtk,bhkv->bhtv', q_c, state)
        state = d_c[..., None] * state + jnp.einsum('bhsk,bhsv->bhkv', ks_c, v_c)
        return state, o

    s0 = jnp.zeros((bsz, GLA_HEADS, GLA_DK, GLA_DV), f32)
    _, o_inter = lax.scan(step, s0, (q_dec.swapaxes(0, 1), k_state.swapaxes(0, 1),
                                     v.swapaxes(0, 1), decay.swapaxes(0, 1)))
    o = o_intra + o_inter.swapaxes(0, 1)
    o = rmsnorm(o, g_o)
    o = o.transpose(0, 1, 3, 2, 4).reshape(bsz, seq, D_INNER).astype(h.dtype)
    return (o * jax.nn.silu(z)) @ w_out


def setup_inputs(seed: int = 0) -> dict:
    key = jax.random.key(seed)
    ks = jax.random.split(key, 20)
    nrm = lambda k, shape, s: jax.random.normal(k, shape, jnp.float32) * s
    return {
        "x": nrm(ks[0], (BATCH, SEQ, D_MODEL), 1.0),
        "c": nrm(ks[1], (BATCH, D_MODEL), 1.0),
        "w_ada": nrm(ks[2], (DEPTH, D_MODEL, 3 * D_MODEL), D_MODEL ** -0.5),
        "b_ada": nrm(ks[3], (DEPTH, 3 * D_MODEL), 0.02),
        "g_norm": 1.0 + nrm(ks[4], (DEPTH, D_MODEL), 0.02),
        "a_w_in": nrm(ks[5], (N_A, D_MODEL, A_IN), D_MODEL ** -0.5),
        "a_w_s": nrm(ks[6], (N_A, SGU_GROUPS, SGU_CHUNK, SGU_CHUNK), SGU_CHUNK ** -0.5),
        "a_b_s": 1.0 + nrm(ks[7], (N_A, SGU_GROUPS, SGU_CHUNK), 0.02),
        "a_g_v": 1.0 + nrm(ks[8], (N_A, D_INNER), 0.02),
        "a_w_out": nrm(ks[9], (N_A, D_INNER, D_MODEL), D_INNER ** -0.5),
        "b_w_in": nrm(ks[10], (N_B, D_MODEL, B_IN), D_MODEL ** -0.5),
        "b_w_gate_up": nrm(ks[11], (N_B, GLA_GATE_RANK, GLA_DK_TOTAL), GLA_GATE_RANK ** -0.5),
        "b_b_gate": nrm(ks[12], (N_B, GLA_DK_TOTAL), 0.1),
        "b_g_o": 1.0 + nrm(ks[13], (N_B, GLA_DV), 0.02),
        "b_w_out": nrm(ks[14], (N_B, D_INNER, D_MODEL), D_INNER ** -0.5),
        "g_final": 1.0 + nrm(ks[15], (D_MODEL,), 0.02),
    }


def reference(x, c, w_ada, b_ada, g_norm, a_w_in, a_w_s, a_b_s, a_g_v, a_w_out,
              b_w_in, b_w_gate_up, b_b_gate, b_g_o, b_w_out, g_final):
    cond = jax.nn.silu(c)
    for layer in range(DEPTH):
        mod = (cond @ w_ada[layer] + b_ada[layer])[:, None, :]
        shift, scale, gate = jnp.split(mod, 3, axis=-1)
        h = rmsnorm(x, g_norm[layer]) * (1 + scale) + shift
        j = layer // N_MIXERS
        if layer % N_MIXERS == 0:
            y = sgu_branch(h, a_w_in[j], a_w_s[j], a_b_s[j], a_g_v[j], a_w_out[j])
        else:
            y = gla_branch(h, b_w_in[j], b_w_gate_up[j], b_b_gate[j], b_g_o[j], b_w_out[j])
        x = x + gate * y
    return rmsnorm(x, g_final)
```

```python
import functools

import jax
import jax.numpy as jnp
from jax import lax
from jax.experimental import pallas as pl
from jax.experimental.pallas import tpu as pltpu

EPS = 1e-6
N_MIXERS = 2

SGU_CHUNK = 128
SGU_GROUPS = 8

GLA_HEADS = 4
GLA_GATE_RANK = 16
GLA_GATE_TAU = 16.0
GLA_CHUNK = 64

LANES = 128
SEQ_TILE = 512
VMEM_LIMIT_BYTES = 56 * 1024 * 1024

F32 = jnp.float32
BF16 = jnp.bfloat16


def _dot(a, b):
    return jnp.dot(a, b, preferred_element_type=F32)


def _dot_nt(a, b):
    return lax.dot_general(a, b, (((1,), (1,)), ((), ())), preferred_element_type=F32)


def _dot_tn(a, b):
    return lax.dot_general(a, b, (((0,), (0,)), ((), ())), preferred_element_type=F32)


def _rms(x):
    return x * lax.rsqrt(jnp.mean(x * x, axis=-1, keepdims=True) + EPS)


def _modulated_norm(x, mod_ref, gn_ref):
    shift = mod_ref[0, 0:1, :]
    scale = mod_ref[0, 1:2, :]
    return (_rms(x) * gn_ref[...]) * (1.0 + scale) + shift


def _mod_kernel(c_ref, w_ref, b_ref, o_ref):
    cond = jax.nn.silu(c_ref[...])
    o_ref[0, 0] = _dot(cond.astype(BF16), w_ref[0].astype(BF16)) + b_ref[0, 0]


def _modulation(c, w_ada, b_ada):
    depth, d, _ = w_ada.shape
    bsz = c.shape[0]
    out = pl.pallas_call(
        _mod_kernel,
        grid=(depth, 3),
        in_specs=[
            pl.BlockSpec((bsz, d), lambda l, j: (0, 0)),
            pl.BlockSpec((1, d, d), lambda l, j: (l, 0, j)),
            pl.BlockSpec((1, 1, 1, d), lambda l, j: (l, j, 0, 0)),
        ],
        out_specs=pl.BlockSpec((1, 1, bsz, d), lambda l, j: (l, j, 0, 0)),
        out_shape=jax.ShapeDtypeStruct((depth, 3, bsz, d), F32),
        compiler_params=pltpu.CompilerParams(dimension_semantics=("arbitrary", "arbitrary")),
        name="adaln_modulation",
    )(c, w_ada, b_ada.reshape(depth, 3, 1, d))
    return out.transpose(0, 2, 1, 3)


def _sgu_layer_kernel(x_ref, mod_ref, gn_ref, wv_ref, wuz_ref, ws_ref, bs_ref, gv_ref, wout_ref,
                      o_ref, h_ref, vn_ref, acc_ref):
    ts = x_ref.shape[1]
    gd = vn_ref.shape[2]
    n_chunks = ts // SGU_CHUNK

    h_ref[...] = _modulated_norm(x_ref[0], mod_ref, gn_ref).astype(BF16)

    v = jax.nn.gelu(_dot(h_ref[...], wv_ref[...]))
    vn = (_rms(v) * gv_ref[...]).astype(BF16)
    for g in range(SGU_GROUPS):
        vn_ref[g] = vn[:, g * gd:(g + 1) * gd]

    acc_ref[...] = jnp.zeros_like(acc_ref)
    row = lax.broadcasted_iota(jnp.int32, (SGU_CHUNK, SGU_CHUNK), 0)
    col = lax.broadcasted_iota(jnp.int32, (SGU_CHUNK, SGU_CHUNK), 1)
    causal = col <= row

    def group(g, carry):
        uz = _dot(h_ref[...], wuz_ref[g])
        u = jax.nn.gelu(uz[:, :gd])
        z = uz[:, gd:]
        w = jnp.where(causal, ws_ref[g], 0.0).astype(BF16)
        bias = bs_ref[g]
        mixed = jnp.concatenate(
            [_dot(w, vn_ref[g, c * SGU_CHUNK:(c + 1) * SGU_CHUNK, :]) + bias for c in range(n_chunks)],
            axis=0)
        y = u * mixed * jax.nn.silu(z)
        acc_ref[...] += _dot(y.astype(BF16), wout_ref[g])
        return carry

    lax.fori_loop(0, SGU_GROUPS, group, 0)
    gate = mod_ref[0, 2:3, :]
    o_ref[0] = x_ref[0] + gate * acc_ref[...]


def _const_spec(shape):
    nd = len(shape)
    return pl.BlockSpec(shape, lambda b, s: (0,) * nd, pipeline_mode=pl.Buffered(1))


def _sgu_layer(x, mod, g_norm, w_in, w_s, b_s, g_v, w_out):
    bsz, seq, d = x.shape
    d_inner = w_out.shape[0]
    gd = d_inner // SGU_GROUPS
    ts = SEQ_TILE
    w_u, w_v, w_z = jnp.split(w_in.astype(BF16), 3, axis=-1)
    w_uz = jnp.concatenate([w_u.reshape(d, SGU_GROUPS, gd), w_z.reshape(d, SGU_GROUPS, gd)], axis=-1)
    w_uz = w_uz.transpose(1, 0, 2)
    w_o = w_out.astype(BF16).reshape(SGU_GROUPS, gd, d)
    return pl.pallas_call(
        _sgu_layer_kernel,
        grid=(bsz, seq // ts),
        in_specs=[
            pl.BlockSpec((1, ts, d), lambda b, s: (b, s, 0)),
            pl.BlockSpec((1, 3, d), lambda b, s: (b, 0, 0)),
            _const_spec((1, d)),
            _const_spec((d, d_inner)),
            _const_spec((SGU_GROUPS, d, 2 * gd)),
            _const_spec((SGU_GROUPS, SGU_CHUNK, SGU_CHUNK)),
            _const_spec((SGU_GROUPS, SGU_CHUNK, 1)),
            _const_spec((1, d_inner)),
            _const_spec((SGU_GROUPS, gd, d)),
        ],
        out_specs=pl.BlockSpec((1, ts, d), lambda b, s: (b, s, 0)),
        out_shape=jax.ShapeDtypeStruct(x.shape, x.dtype),
        scratch_shapes=[
            pltpu.VMEM((ts, d), BF16),
            pltpu.VMEM((SGU_GROUPS, ts, gd), BF16),
            pltpu.VMEM((ts, d), F32),
        ],
        compiler_params=pltpu.CompilerParams(
            dimension_semantics=("arbitrary", "arbitrary"), vmem_limit_bytes=VMEM_LIMIT_BYTES),
        name="sgu_layer",
    )(x, mod, g_norm.reshape(1, d), w_v, w_uz, w_s, b_s.reshape(SGU_GROUPS, SGU_CHUNK, 1),
      g_v.reshape(1, d_inner), w_o)


def _gla_layer_kernel(x_ref, mod_ref, gn_ref, wqk_ref, wv_ref, wz_ref, wa_ref, wg_ref, bg_ref, go_ref,
                      wout_ref, gf_ref, o_ref,
                      h_ref, qd_ref, ki_ref, ks_ref, v_ref, dec_ref, on_ref, state_ref, *, final_norm):
    ts = x_ref.shape[1]
    dkt = qd_ref.shape[1]
    dk = dkt // GLA_HEADS
    dv = v_ref.shape[1] // GLA_HEADS
    n_chunks = ts // GLA_CHUNK
    L = GLA_CHUNK

    @pl.when(pl.program_id(1) == 0)
    def _():
        state_ref[...] = jnp.zeros_like(state_ref)

    h_ref[...] = _modulated_norm(x_ref[0], mod_ref, gn_ref).astype(BF16)
    hb = h_ref[...]

    qk = _dot(hb, wqk_ref[...])
    v_ref[...] = _dot(hb, wv_ref[...]).astype(BF16)
    a_lr = _dot(hb, wa_ref[...])
    pre = _dot(a_lr.astype(BF16), wg_ref[...]) + bg_ref[...]
    log_a = jax.nn.log_sigmoid(pre) / GLA_GATE_TAU

    hi = log_a.astype(BF16)
    lo = (log_a - hi.astype(F32)).astype(BF16)
    row = lax.broadcasted_iota(jnp.int32, (L, L), 0)
    col = lax.broadcasted_iota(jnp.int32, (L, L), 1)
    causal = col <= row
    tri = jnp.where(causal, 1.0, 0.0).astype(BF16)
    tri2 = jnp.concatenate([tri, tri], axis=1)
    b = jnp.concatenate(
        [_dot(tri2, jnp.concatenate([hi[c * L:(c + 1) * L], lo[c * L:(c + 1) * L]], axis=0))
         for c in range(n_chunks)], axis=0)

    b3 = b.reshape(n_chunks, L, dkt)
    b_last = b3[:, L - 1:L, :]
    q = qk[:, :dkt] * (dk ** -0.5)
    k = qk[:, dkt:]
    qd_ref[...] = (q * jnp.exp(b)).astype(BF16)
    ki_ref[...] = (k * jnp.exp(-b)).astype(BF16)
    ks_ref[...] = (k.reshape(n_chunks, L, dkt) * jnp.exp(b_last - b3)).reshape(ts, dkt).astype(BF16)
    dec_ref[...] = jnp.broadcast_to(jnp.exp(b_last), dec_ref.shape)

    def chunk(c, carry):
        r = pl.ds(pl.multiple_of(c * L, L), L)
        for hd in range(GLA_HEADS):
            kc = slice(hd * dk, (hd + 1) * dk)
            vc = slice(hd * dv, (hd + 1) * dv)
            qd = qd_ref[r, kc]
            vch = v_ref[r, vc]
            attn = jnp.where(causal, _dot_nt(qd, ki_ref[r, kc]), 0.0).astype(BF16)
            st = state_ref[hd]
            o = _dot(attn, vch) + _dot(qd, st.astype(BF16))
            dcol = jnp.broadcast_to(dec_ref[c][0:1, kc], (dk, dk)).T
            state_ref[hd] = jnp.tile(dcol, (1, dv // dk)) * st + _dot_tn(ks_ref[r, kc], vch)
            on_ref[r, vc] = _rms(o) * go_ref[...]
        return carry

    lax.fori_loop(0, n_chunks, chunk, 0)

    z = _dot(h_ref[...], wz_ref[...])
    y = (on_ref[...] * jax.nn.silu(z)).astype(BF16)
    gate = mod_ref[0, 2:3, :]
    x_new = x_ref[0] + gate * _dot(y, wout_ref[...])
    if final_norm:
        x_new = _rms(x_new) * gf_ref[...]
    o_ref[0] = x_new


def _gla_layer(x, mod, g_norm, w_in, w_gate_up, b_gate, g_o, w_out, g_final, *, final_norm):
    bsz, seq, d = x.shape
    d_inner = w_out.shape[0]
    dkt = w_gate_up.shape[1]
    dv = d_inner // GLA_HEADS
    ts = SEQ_TILE
    w_b = w_in.astype(BF16)
    w_qk = w_b[:, :2 * dkt]
    w_v = w_b[:, 2 * dkt:2 * dkt + d_inner]
    w_z = w_b[:, 2 * dkt + d_inner:2 * dkt + 2 * d_inner]
    w_a = jnp.pad(w_b[:, 2 * dkt + 2 * d_inner:], ((0, 0), (0, LANES - GLA_GATE_RANK)))
    w_g = jnp.pad(w_gate_up.astype(BF16), ((0, LANES - GLA_GATE_RANK), (0, 0)))
    kernel = functools.partial(_gla_layer_kernel, final_norm=final_norm)
    return pl.pallas_call(
        kernel,
        grid=(bsz, seq // ts),
        in_specs=[
            pl.BlockSpec((1, ts, d), lambda b, s: (b, s, 0)),
            pl.BlockSpec((1, 3, d), lambda b, s: (b, 0, 0)),
            _const_spec((1, d)),
            _const_spec((d, 2 * dkt)),
            _const_spec((d, d_inner)),
            _const_spec((d, d_inner)),
            _const_spec((d, LANES)),
            _const_spec((LANES, dkt)),
            _const_spec((1, dkt)),
            _const_spec((1, dv)),
            _const_spec((d_inner, d)),
            _const_spec((1, d)),
        ],
        out_specs=pl.BlockSpec((1, ts, d), lambda b, s: (b, s, 0)),
        out_shape=jax.ShapeDtypeStruct(x.shape, x.dtype),
        scratch_shapes=[
            pltpu.VMEM((ts, d), BF16),
            pltpu.VMEM((ts, dkt), BF16),
            pltpu.VMEM((ts, dkt), BF16),
            pltpu.VMEM((ts, dkt), BF16),
            pltpu.VMEM((ts, d_inner), BF16),
            pltpu.VMEM((ts // GLA_CHUNK, 8, dkt), F32),
            pltpu.VMEM((ts, d_inner), F32),
            pltpu.VMEM((GLA_HEADS, dkt // GLA_HEADS, dv), F32),
        ],
        compiler_params=pltpu.CompilerParams(
            dimension_semantics=("arbitrary", "arbitrary"), vmem_limit_bytes=VMEM_LIMIT_BYTES),
        name="gla_layer_final" if final_norm else "gla_layer",
    )(x, mod, g_norm.reshape(1, d), w_qk, w_v, w_z, w_a, w_g, b_gate.reshape(1, dkt),
      g_o.reshape(1, dv), w_out.astype(BF16), g_final.reshape(1, d))


def kernel(x, c, w_ada, b_ada, g_norm, a_w_in, a_w_s, a_b_s, a_g_v, a_w_out,
           b_w_in, b_w_gate_up, b_b_gate, b_g_o, b_w_out, g_final):
    depth = w_ada.shape[0]
    assert depth % N_MIXERS == 0, "the final norm is fused into a trailing mixer-B layer"
    mod = _modulation(c, w_ada, b_ada)
    for layer in range(depth):
        j = layer // N_MIXERS
        if layer % N_MIXERS == 0:
            x = _sgu_layer(x, mod[layer], g_norm[layer], a_w_in[j], a_w_s[j], a_b_s[j], a_g_v[j], a_w_out[j])
        else:
            x = _gla_layer(x, mod[layer], g_norm[layer], b_w_in[j], b_w_gate_up[j], b_b_gate[j], b_g_o[j],
                           b_w_out[j], g_final, final_norm=(layer == depth - 1))
    return x
```

```python
import functools

import jax
import jax.numpy as jnp
from jax import lax
from jax.experimental import pallas as pl
from jax.experimental.pallas import tpu as pltpu

EPS = 1e-6
N_MIXERS = 2

SGU_CHUNK = 128
SGU_GROUPS = 8

GLA_HEADS = 4
GLA_GATE_RANK = 16
GLA_GATE_TAU = 16.0
GLA_CHUNK = 64

LANES = 128
SEQ_TILE = 512
VMEM_LIMIT_BYTES = 56 * 1024 * 1024

F32 = jnp.float32
BF16 = jnp.bfloat16


def _dot(a, b):
    return jnp.dot(a, b, preferred_element_type=F32)


def _dot_nt(a, b):
    return lax.dot_general(a, b, (((1,), (1,)), ((), ())), preferred_element_type=F32)


def _dot_tn(a, b):
    return lax.dot_general(a, b, (((0,), (0,)), ((), ())), preferred_element_type=F32)


def _rms(x):
    return x * lax.rsqrt(jnp.mean(x * x, axis=-1, keepdims=True) + EPS)


def _modulated_norm(x, mod_ref, gn_ref):
    shift = mod_ref[0, 0:1, :]
    scale = mod_ref[0, 1:2, :]
    return (_rms(x) * gn_ref[...]) * (1.0 + scale) + shift


def _mod_kernel(c_ref, w_ref, b_ref, o_ref):
    cond = jax.nn.silu(c_ref[...])
    o_ref[0, 0] = _dot(cond.astype(BF16), w_ref[0].astype(BF16)) + b_ref[0, 0]


def _modulation(c, w_ada, b_ada):
    depth, d, _ = w_ada.shape
    bsz = c.shape[0]
    out = pl.pallas_call(
        _mod_kernel,
        grid=(depth, 3),
        in_specs=[
            pl.BlockSpec((bsz, d), lambda l, j: (0, 0)),
            pl.BlockSpec((1, d, d), lambda l, j: (l, 0, j)),
            pl.BlockSpec((1, 1, 1, d), lambda l, j: (l, j, 0, 0)),
        ],
        out_specs=pl.BlockSpec((1, 1, bsz, d), lambda l, j: (l, j, 0, 0)),
        out_shape=jax.ShapeDtypeStruct((depth, 3, bsz, d), F32),
        compiler_params=pltpu.CompilerParams(dimension_semantics=("arbitrary", "arbitrary")),
        name="adaln_modulation",
    )(c, w_ada, b_ada.reshape(depth, 3, 1, d))
    return out.transpose(0, 2, 1, 3)


def _sgu_layer_kernel(x_ref, mod_ref, gn_ref, wv_ref, wuz_ref, ws_ref, bs_ref, gv_ref, wout_ref,
                      o_ref, h_ref, vn_ref, acc_ref):
    ts = x_ref.shape[1]
    gd = vn_ref.shape[2]
    n_chunks = ts // SGU_CHUNK

    h_ref[...] = _modulated_norm(x_ref[0], mod_ref, gn_ref).astype(BF16)

    v = jax.nn.gelu(_dot(h_ref[...], wv_ref[...]))
    vn = (_rms(v) * gv_ref[...]).astype(BF16)
    for g in range(SGU_GROUPS):
        vn_ref[g] = vn[:, g * gd:(g + 1) * gd]

    acc_ref[...] = jnp.zeros_like(acc_ref)
    row = lax.broadcasted_iota(jnp.int32, (SGU_CHUNK, SGU_CHUNK), 0)
    col = lax.broadcasted_iota(jnp.int32, (SGU_CHUNK, SGU_CHUNK), 1)
    causal = col <= row

    def group(g, carry):
        uz = _dot(h_ref[...], wuz_ref[g])
        u = jax.nn.gelu(uz[:, :gd])
        z = uz[:, gd:]
        w = jnp.where(causal, ws_ref[g], 0.0).astype(BF16)
        bias = bs_ref[g]
        mixed = jnp.concatenate(
            [_dot(w, vn_ref[g, c * SGU_CHUNK:(c + 1) * SGU_CHUNK, :]) + bias for c in range(n_chunks)],
            axis=0)
        y = u * mixed * jax.nn.silu(z)
        acc_ref[...] += _dot(y.astype(BF16), wout_ref[g])
        return carry

    lax.fori_loop(0, SGU_GROUPS, group, 0, unroll=True)
    gate = mod_ref[0, 2:3, :]
    o_ref[0] = x_ref[0] + gate * acc_ref[...]


def _const_spec(shape):
    nd = len(shape)
    return pl.BlockSpec(shape, lambda b, s: (0,) * nd, pipeline_mode=pl.Buffered(1))


def _sgu_layer(x, mod, g_norm, w_in, w_s, b_s, g_v, w_out):
    bsz, seq, d = x.shape
    d_inner = w_out.shape[0]
    gd = d_inner // SGU_GROUPS
    ts = SEQ_TILE
    w_u, w_v, w_z = jnp.split(w_in.astype(BF16), 3, axis=-1)
    w_uz = jnp.concatenate([w_u.reshape(d, SGU_GROUPS, gd), w_z.reshape(d, SGU_GROUPS, gd)], axis=-1)
    w_uz = w_uz.transpose(1, 0, 2)
    w_o = w_out.astype(BF16).reshape(SGU_GROUPS, gd, d)
    return pl.pallas_call(
        _sgu_layer_kernel,
        grid=(bsz, seq // ts),
        in_specs=[
            pl.BlockSpec((1, ts, d), lambda b, s: (b, s, 0)),
            pl.BlockSpec((1, 3, d), lambda b, s: (b, 0, 0)),
            _const_spec((1, d)),
            _const_spec((d, d_inner)),
            _const_spec((SGU_GROUPS, d, 2 * gd)),
            _const_spec((SGU_GROUPS, SGU_CHUNK, SGU_CHUNK)),
            _const_spec((SGU_GROUPS, SGU_CHUNK, 1)),
            _const_spec((1, d_inner)),
            _const_spec((SGU_GROUPS, gd, d)),
        ],
        out_specs=pl.BlockSpec((1, ts, d), lambda b, s: (b, s, 0)),
        out_shape=jax.ShapeDtypeStruct(x.shape, x.dtype),
        scratch_shapes=[
            pltpu.VMEM((ts, d), BF16),
            pltpu.VMEM((SGU_GROUPS, ts, gd), BF16),
            pltpu.VMEM((ts, d), F32),
        ],
        compiler_params=pltpu.CompilerParams(
            dimension_semantics=("arbitrary", "arbitrary"), vmem_limit_bytes=VMEM_LIMIT_BYTES),
        name="sgu_layer",
    )(x, mod, g_norm.reshape(1, d), w_v, w_uz, w_s, b_s.reshape(SGU_GROUPS, SGU_CHUNK, 1),
      g_v.reshape(1, d_inner), w_o)


def _gla_layer_kernel(x_ref, mod_ref, gn_ref, wqk_ref, wv_ref, wz_ref, wa_ref, wg_ref, bg_ref, go_ref,
                      wout_ref, gf_ref, o_ref,
                      h_ref, qd_ref, qp_ref, kcat_ref, kst_ref, v_ref, dec_ref, on_ref, state_ref, *, final_norm):
    ts = x_ref.shape[1]
    dkt = qd_ref.shape[1]
    dk = dkt // GLA_HEADS
    dv = v_ref.shape[1] // GLA_HEADS
    L = GLA_CHUNK
    blk = 2 * L
    n_blk = ts // blk

    @pl.when(pl.program_id(1) == 0)
    def _():
        state_ref[...] = jnp.zeros_like(state_ref)

    h_ref[...] = _modulated_norm(x_ref[0], mod_ref, gn_ref).astype(BF16)
    hb = h_ref[...]

    qk = _dot(hb, wqk_ref[...])
    v_ref[...] = _dot(hb, wv_ref[...]).astype(BF16)
    a_lr = _dot(hb, wa_ref[...])
    pre = _dot(a_lr.astype(BF16), wg_ref[...]) + bg_ref[...]
    log_a = jax.nn.log_sigmoid(pre) / GLA_GATE_TAU

    hi = log_a.astype(BF16)
    lo = (log_a - hi.astype(F32)).astype(BF16)
    row = lax.broadcasted_iota(jnp.int32, (blk, blk), 0)
    col = lax.broadcasted_iota(jnp.int32, (blk, blk), 1)
    causal = col <= row
    tri = jnp.where(causal & ((row >= L) == (col >= L)), 1.0, 0.0).astype(BF16)
    tri2 = jnp.concatenate([tri, tri], axis=1)
    b = jnp.concatenate(
        [_dot(tri2, jnp.concatenate([hi[i * blk:(i + 1) * blk], lo[i * blk:(i + 1) * blk]], axis=0))
         for i in range(n_blk)], axis=0)

    b4 = b.reshape(n_blk, 2, L, dkt)
    bl_lo = b4[:, 0:1, L - 1:L, :]
    bl_hi = b4[:, 1:2, L - 1:L, :]
    is_lo = lax.broadcasted_iota(jnp.int32, (1, 2, 1, 1), 1) == 0
    lo_on_hi = jnp.where(is_lo, 0.0, bl_lo)
    lo_on_lo = jnp.where(is_lo, bl_lo, 0.0)
    q4 = (qk[:, :dkt] * (dk ** -0.5)).reshape(b4.shape)
    k4 = qk[:, dkt:].reshape(b4.shape)
    flat = lambda t: t.reshape(ts, dkt)
    qd_ref[...] = flat(q4 * jnp.exp(b4)).astype(BF16)
    qp_ref[...] = flat(q4 * jnp.exp(b4 + lo_on_hi)).astype(BF16)
    k_mix = (k4 * jnp.exp(lo_on_lo - b4)).reshape(n_blk, blk, dkt)
    k_inv = (k4 * jnp.exp(-b4)).reshape(n_blk, blk, dkt)
    kcat_ref[...] = jnp.concatenate([k_mix, k_inv], axis=1).astype(BF16)
    kst_ref[...] = flat(k4 * jnp.exp(bl_hi + lo_on_lo - b4)).T.astype(BF16)
    dec_ref[...] = jnp.broadcast_to(jnp.exp(bl_lo + bl_hi).reshape(n_blk, 1, dkt), dec_ref.shape)

    for i in range(n_blk):
        r = slice(i * blk, (i + 1) * blk)
        for hd in range(GLA_HEADS):
            kc = slice(hd * dk, (hd + 1) * dk)
            vc = slice(hd * dv, (hd + 1) * dv)
            s = _dot_nt(qd_ref[r, kc], kcat_ref[i, :, kc])
            s = jnp.concatenate([s[:L, blk:], s[L:, :blk]], axis=0)
            attn = jnp.where(causal, s, 0.0).astype(BF16)
            vb = v_ref[r, vc]
            ou = _dot(jnp.concatenate([attn, kst_ref[kc, r]], axis=0), vb)
            st = state_ref[hd]
            o = ou[:blk] + _dot(qp_ref[r, kc], st.astype(BF16))
            dcol = jnp.broadcast_to(dec_ref[i, 0:1, kc], (dk, dk)).T
            state_ref[hd] = jnp.tile(dcol, (1, dv // dk)) * st + ou[blk:]
            on_ref[r, vc] = _rms(o) * go_ref[...]

    z = _dot(h_ref[...], wz_ref[...])
    y = (on_ref[...] * jax.nn.silu(z)).astype(BF16)
    gate = mod_ref[0, 2:3, :]
    x_new = x_ref[0] + gate * _dot(y, wout_ref[...])
    if final_norm:
        x_new = _rms(x_new) * gf_ref[...]
    o_ref[0] = x_new


def _gla_layer(x, mod, g_norm, w_in, w_gate_up, b_gate, g_o, w_out, g_final, *, final_norm):
    bsz, seq, d = x.shape
    d_inner = w_out.shape[0]
    dkt = w_gate_up.shape[1]
    dv = d_inner // GLA_HEADS
    ts = SEQ_TILE
    blk = 2 * GLA_CHUNK
    w_b = w_in.astype(BF16)
    w_qk = w_b[:, :2 * dkt]
    w_v = w_b[:, 2 * dkt:2 * dkt + d_inner]
    w_z = w_b[:, 2 * dkt + d_inner:2 * dkt + 2 * d_inner]
    w_a = jnp.pad(w_b[:, 2 * dkt + 2 * d_inner:], ((0, 0), (0, LANES - GLA_GATE_RANK)))
    w_g = jnp.pad(w_gate_up.astype(BF16), ((0, LANES - GLA_GATE_RANK), (0, 0)))
    kernel = functools.partial(_gla_layer_kernel, final_norm=final_norm)
    return pl.pallas_call(
        kernel,
        grid=(bsz, seq // ts),
        in_specs=[
            pl.BlockSpec((1, ts, d), lambda b, s: (b, s, 0)),
            pl.BlockSpec((1, 3, d), lambda b, s: (b, 0, 0)),
            _const_spec((1, d)),
            _const_spec((d, 2 * dkt)),
            _const_spec((d, d_inner)),
            _const_spec((d, d_inner)),
            _const_spec((d, LANES)),
            _const_spec((LANES, dkt)),
            _const_spec((1, dkt)),
            _const_spec((1, dv)),
            _const_spec((d_inner, d)),
            _const_spec((1, d)),
        ],
        out_specs=pl.BlockSpec((1, ts, d), lambda b, s: (b, s, 0)),
        out_shape=jax.ShapeDtypeStruct(x.shape, x.dtype),
        scratch_shapes=[
            pltpu.VMEM((ts, d), BF16),
            pltpu.VMEM((ts, dkt), BF16),
            pltpu.VMEM((ts, dkt), BF16),
            pltpu.VMEM((ts // blk, 2 * blk, dkt), BF16),
            pltpu.VMEM((dkt, ts), BF16),
            pltpu.VMEM((ts, d_inner), BF16),
            pltpu.VMEM((ts // blk, 8, dkt), F32),
            pltpu.VMEM((ts, d_inner), F32),
            pltpu.VMEM((GLA_HEADS, dkt // GLA_HEADS, dv), F32),
        ],
        compiler_params=pltpu.CompilerParams(
            dimension_semantics=("arbitrary", "arbitrary"), vmem_limit_bytes=VMEM_LIMIT_BYTES),
        name="gla_layer_final" if final_norm else "gla_layer",
    )(x, mod, g_norm.reshape(1, d), w_qk, w_v, w_z, w_a, w_g, b_gate.reshape(1, dkt),
      g_o.reshape(1, dv), w_out.astype(BF16), g_final.reshape(1, d))


def kernel(x, c, w_ada, b_ada, g_norm, a_w_in, a_w_s, a_b_s, a_g_v, a_w_out,
           b_w_in, b_w_gate_up, b_b_gate, b_g_o, b_w_out, g_final):
    depth = w_ada.shape[0]
    assert depth % N_MIXERS == 0, "the final norm is fused into a trailing mixer-B layer"
    mod = _modulation(c, w_ada, b_ada)
    for layer in range(depth):
        j = layer // N_MIXERS
        if layer % N_MIXERS == 0:
            x = _sgu_layer(x, mod[layer], g_norm[layer], a_w_in[j], a_w_s[j], a_b_s[j], a_g_v[j], a_w_out[j])
        else:
            x = _gla_layer(x, mod[layer], g_norm[layer], b_w_in[j], b_w_gate_up[j], b_b_gate[j], b_g_o[j],
                           b_w_out[j], g_final, final_norm=(layer == depth - 1))
    return x
```

```python
import functools

import jax
import jax.numpy as jnp
from jax import lax
from jax.experimental import pallas as pl
from jax.experimental.pallas import tpu as pltpu

EPS = 1e-6
N_MIXERS = 2

SGU_CHUNK = 128
SGU_GROUPS = 8

GLA_HEADS = 4
GLA_GATE_RANK = 16
GLA_GATE_TAU = 16.0
GLA_CHUNK = 64

LANES = 128
SEQ_TILE = 512
VMEM_LIMIT_BYTES = 56 * 1024 * 1024

F32 = jnp.float32
BF16 = jnp.bfloat16


def _dot(a, b):
    return jnp.dot(a, b, preferred_element_type=F32)


def _dot_nt(a, b):
    return lax.dot_general(a, b, (((1,), (1,)), ((), ())), preferred_element_type=F32)


def _dot_tn(a, b):
    return lax.dot_general(a, b, (((0,), (0,)), ((), ())), preferred_element_type=F32)


def _rms(x):
    return x * lax.rsqrt(jnp.mean(x * x, axis=-1, keepdims=True) + EPS)


def _modulated_norm(x, mod_ref, gn_ref):
    shift = mod_ref[0, 0:1, :]
    scale = mod_ref[0, 1:2, :]
    return (_rms(x) * gn_ref[...]) * (1.0 + scale) + shift


def _mod_kernel(c_ref, w_ref, b_ref, o_ref):
    cond = jax.nn.silu(c_ref[...])
    o_ref[0, 0] = _dot(cond.astype(BF16), w_ref[0].astype(BF16)) + b_ref[0, 0]


def _modulation(c, w_ada, b_ada):
    depth, d, _ = w_ada.shape
    bsz = c.shape[0]
    out = pl.pallas_call(
        _mod_kernel,
        grid=(depth, 3),
        in_specs=[
            pl.BlockSpec((bsz, d), lambda l, j: (0, 0)),
            pl.BlockSpec((1, d, d), lambda l, j: (l, 0, j)),
            pl.BlockSpec((1, 1, 1, d), lambda l, j: (l, j, 0, 0)),
        ],
        out_specs=pl.BlockSpec((1, 1, bsz, d), lambda l, j: (l, j, 0, 0)),
        out_shape=jax.ShapeDtypeStruct((depth, 3, bsz, d), F32),
        compiler_params=pltpu.CompilerParams(dimension_semantics=("arbitrary", "arbitrary")),
        name="adaln_modulation",
    )(c, w_ada, b_ada.reshape(depth, 3, 1, d))
    return out.transpose(0, 2, 1, 3)


def _sgu_layer_kernel(x_ref, mod_ref, gn_ref, wv_ref, wuz_ref, ws_ref, bs_ref, gv_ref, wout_ref,
                      o_ref, h_ref, vn_ref, y_ref):
    ts = x_ref.shape[1]
    gd = vn_ref.shape[2]
    n_chunks = ts // SGU_CHUNK

    h_ref[...] = _modulated_norm(x_ref[0], mod_ref, gn_ref).astype(BF16)

    v = _dot(h_ref[...], wv_ref[...])
    uz_next = _dot(h_ref[...], wuz_ref[0])
    vn = (_rms(jax.nn.gelu(v)) * gv_ref[...]).astype(BF16)
    for g in range(SGU_GROUPS):
        vn_ref[g] = vn[:, g * gd:(g + 1) * gd]

    row = lax.broadcasted_iota(jnp.int32, (SGU_CHUNK, SGU_CHUNK), 0)
    col = lax.broadcasted_iota(jnp.int32, (SGU_CHUNK, SGU_CHUNK), 1)
    causal = col <= row

    def mix(g):
        w = jnp.where(causal, ws_ref[g], 0.0).astype(BF16)
        bias = bs_ref[g]
        return jnp.concatenate(
            [_dot(w, vn_ref[g, c * SGU_CHUNK:(c + 1) * SGU_CHUNK, :]) + bias for c in range(n_chunks)],
            axis=0)

    mixed_next = mix(0)
    for g in range(SGU_GROUPS):
        uz, mixed = uz_next, mixed_next
        if g + 1 < SGU_GROUPS:
            uz_next = _dot(h_ref[...], wuz_ref[g + 1])
            mixed_next = mix(g + 1)
        y = jax.nn.gelu(uz[:, :gd]) * mixed * jax.nn.silu(uz[:, gd:])
        y_ref[:, g * gd:(g + 1) * gd] = y.astype(BF16)
    gate = mod_ref[0, 2:3, :]
    o_ref[0] = x_ref[0] + gate * _dot(y_ref[...], wout_ref[...])


def _const_spec(shape):
    nd = len(shape)
    return pl.BlockSpec(shape, lambda b, s: (0,) * nd, pipeline_mode=pl.Buffered(1))


def _sgu_layer(x, mod, g_norm, w_in, w_s, b_s, g_v, w_out):
    bsz, seq, d = x.shape
    d_inner = w_out.shape[0]
    gd = d_inner // SGU_GROUPS
    ts = SEQ_TILE
    w_u, w_v, w_z = jnp.split(w_in.astype(BF16), 3, axis=-1)
    w_uz = jnp.concatenate([w_u.reshape(d, SGU_GROUPS, gd), w_z.reshape(d, SGU_GROUPS, gd)], axis=-1)
    w_uz = w_uz.transpose(1, 0, 2)
    w_o = w_out.astype(BF16)
    return pl.pallas_call(
        _sgu_layer_kernel,
        grid=(bsz, seq // ts),
        in_specs=[
            pl.BlockSpec((1, ts, d), lambda b, s: (b, s, 0)),
            pl.BlockSpec((1, 3, d), lambda b, s: (b, 0, 0)),
            _const_spec((1, d)),
            _const_spec((d, d_inner)),
            _const_spec((SGU_GROUPS, d, 2 * gd)),
            _const_spec((SGU_GROUPS, SGU_CHUNK, SGU_CHUNK)),
            _const_spec((SGU_GROUPS, SGU_CHUNK, 1)),
            _const_spec((1, d_inner)),
            _const_spec((d_inner, d)),
        ],
        out_specs=pl.BlockSpec((1, ts, d), lambda b, s: (b, s, 0)),
        out_shape=jax.ShapeDtypeStruct(x.shape, x.dtype),
        scratch_shapes=[
            pltpu.VMEM((ts, d), BF16),
            pltpu.VMEM((SGU_GROUPS, ts, gd), BF16),
            pltpu.VMEM((ts, d_inner), BF16),
        ],
        compiler_params=pltpu.CompilerParams(
            dimension_semantics=("arbitrary", "arbitrary"), vmem_limit_bytes=VMEM_LIMIT_BYTES),
        name="sgu_layer",
    )(x, mod, g_norm.reshape(1, d), w_v, w_uz, w_s, b_s.reshape(SGU_GROUPS, SGU_CHUNK, 1),
      g_v.reshape(1, d_inner), w_o)


def _gla_layer_kernel(x_ref, mod_ref, gn_ref, wqk_ref, wv_ref, wz_ref, wa_ref, wg_ref, bg_ref, go_ref,
                      wout_ref, gf_ref, o_ref,
                      wag_ref, h_ref, qd_ref, qp_ref, kcat_ref, kst_ref, v_ref, dec_ref, on_ref, upd_ref, zs_ref, y_ref, state_ref,
                      *, final_norm):
    ts = x_ref.shape[1]
    dkt = qd_ref.shape[1]
    dk = dkt // GLA_HEADS
    d_inner = v_ref.shape[1]
    dv = d_inner // GLA_HEADS
    L = GLA_CHUNK
    blk = 2 * L
    n_blk = ts // blk

    @pl.when(pl.program_id(1) == 0)
    def _():
        state_ref[...] = jnp.zeros_like(state_ref)

    @pl.when((pl.program_id(0) == 0) & (pl.program_id(1) == 0))
    def _():
        wag_ref[...] = _dot(wa_ref[...], wg_ref[...]).astype(BF16)

    h_ref[...] = _modulated_norm(x_ref[0], mod_ref, gn_ref).astype(BF16)

    go_all = jnp.tile(go_ref[...], (1, GLA_HEADS))

    def zproj(c0, c1):
        zs_ref[:, c0:c1] = jax.nn.silu(_dot(h_ref[...], wz_ref[:, c0:c1])) * go_all[:, c0:c1]

    qk = _dot(h_ref[...], wqk_ref[...])
    pre = _dot(h_ref[...], wag_ref[...]) + bg_ref[...]
    v_ref[...] = _dot(h_ref[...], wv_ref[...]).astype(BF16)
    log_a = jax.nn.log_sigmoid(pre) / GLA_GATE_TAU

    hi = log_a.astype(BF16)
    lo = (log_a - hi.astype(F32)).astype(BF16)
    row = lax.broadcasted_iota(jnp.int32, (blk, blk), 0)
    col = lax.broadcasted_iota(jnp.int32, (blk, blk), 1)
    causal = col <= row
    tri = jnp.where(causal & ((row >= L) == (col >= L)), 1.0, 0.0).astype(BF16)
    tri2 = jnp.concatenate([tri, tri], axis=1)
    b = jnp.concatenate(
        [_dot(tri2, jnp.concatenate([hi[i * blk:(i + 1) * blk], lo[i * blk:(i + 1) * blk]], axis=0))
         for i in range(n_blk)], axis=0)
    zproj(0, d_inner // 2)

    b4 = b.reshape(n_blk, 2, L, dkt)
    bl_lo = b4[:, 0:1, L - 1:L, :]
    bl_hi = b4[:, 1:2, L - 1:L, :]
    is_lo = lax.broadcasted_iota(jnp.int32, (1, 2, 1, 1), 1) == 0
    lo_on_hi = jnp.where(is_lo, 0.0, bl_lo)
    lo_on_lo = jnp.where(is_lo, bl_lo, 0.0)
    q4 = (qk[:, :dkt] * (dk ** -0.5)).reshape(b4.shape)
    k4 = qk[:, dkt:].reshape(b4.shape)
    flat = lambda t: t.reshape(ts, dkt)
    qd_ref[...] = flat(q4 * jnp.exp(b4)).astype(BF16)
    qp_ref[...] = flat(q4 * jnp.exp(b4 + lo_on_hi)).astype(BF16)
    k_mix = (k4 * jnp.exp(lo_on_lo - b4)).reshape(n_blk, blk, dkt)
    k_inv = (k4 * jnp.exp(-b4)).reshape(n_blk, blk, dkt)
    kcat_ref[...] = jnp.concatenate([k_mix, k_inv], axis=1).astype(BF16)
    kst_ref[...] = flat(k4 * jnp.exp(bl_hi + lo_on_lo - b4)).T.astype(BF16)
    dec_ref[...] = jnp.broadcast_to(jnp.exp(bl_lo + bl_hi).reshape(n_blk, 1, dkt), dec_ref.shape)

    blocks = [(i, hd) for i in range(n_blk) for hd in range(GLA_HEADS)]
    rows = lambda i: slice(i * blk, (i + 1) * blk)
    kcols = lambda hd: slice(hd * dk, (hd + 1) * dk)
    vcols = lambda hd: slice(hd * dv, (hd + 1) * dv)

    attn = {}
    for i, hd in blocks:
        s = _dot_nt(qd_ref[rows(i), kcols(hd)], kcat_ref[i, :, kcols(hd)])
        s = jnp.concatenate([s[:L, blk:], s[L:, :blk]], axis=0)
        attn[i, hd] = jnp.where(causal, s, 0.0).astype(BF16)
    zproj(d_inner // 2, 3 * d_inner // 4)

    for i, hd in blocks:
        lhs = jnp.concatenate([attn[i, hd], kst_ref[kcols(hd), rows(i)]], axis=0)
        ou = _dot(lhs, v_ref[rows(i), vcols(hd)])
        on_ref[rows(i), vcols(hd)] = ou[:blk]
        upd_ref[i * GLA_HEADS + hd] = ou[blk:]

    for hd in range(GLA_HEADS):
        if hd == 1:
            zproj(3 * d_inner // 4, d_inner)
        for i in range(n_blk):
            st = state_ref[hd]
            o = on_ref[rows(i), vcols(hd)] + _dot(qp_ref[rows(i), kcols(hd)], st.astype(BF16))
            dcol = jnp.broadcast_to(dec_ref[i, 0:1, kcols(hd)], (dk, dk)).T
            state_ref[hd] = jnp.tile(dcol, (1, dv // dk)) * st + upd_ref[i * GLA_HEADS + hd]
            y_ref[rows(i), vcols(hd)] = (_rms(o) * zs_ref[rows(i), vcols(hd)]).astype(BF16)

    gate = mod_ref[0, 2:3, :]
    x_new = x_ref[0] + gate * _dot(y_ref[...], wout_ref[...])
    if final_norm:
        x_new = _rms(x_new) * gf_ref[...]
    o_ref[0] = x_new


def _gla_layer(x, mod, g_norm, w_in, w_gate_up, b_gate, g_o, w_out, g_final, *, final_norm):
    bsz, seq, d = x.shape
    d_inner = w_out.shape[0]
    dkt = w_gate_up.shape[1]
    dv = d_inner // GLA_HEADS
    ts = SEQ_TILE
    blk = 2 * GLA_CHUNK
    w_b = w_in.astype(BF16)
    w_v = w_b[:, 2 * dkt:2 * dkt + d_inner]
    w_z = w_b[:, 2 * dkt + d_inner:2 * dkt + 2 * d_inner]
    w_qk = w_b[:, :2 * dkt]
    w_a = jnp.pad(w_b[:, 2 * dkt + 2 * d_inner:], ((0, 0), (0, LANES - GLA_GATE_RANK)))
    w_g = jnp.pad(w_gate_up.astype(BF16), ((0, LANES - GLA_GATE_RANK), (0, 0)))
    kernel = functools.partial(_gla_layer_kernel, final_norm=final_norm)
    return pl.pallas_call(
        kernel,
        grid=(bsz, seq // ts),
        in_specs=[
            pl.BlockSpec((1, ts, d), lambda b, s: (b, s, 0)),
            pl.BlockSpec((1, 3, d), lambda b, s: (b, 0, 0)),
            _const_spec((1, d)),
            _const_spec((d, 2 * dkt)),
            _const_spec((d, d_inner)),
            _const_spec((d, d_inner)),
            _const_spec((d, LANES)),
            _const_spec((LANES, dkt)),
            _const_spec((1, dkt)),
            _const_spec((1, dv)),
            _const_spec((d_inner, d)),
            _const_spec((1, d)),
        ],
        out_specs=pl.BlockSpec((1, ts, d), lambda b, s: (b, s, 0)),
        out_shape=jax.ShapeDtypeStruct(x.shape, x.dtype),
        scratch_shapes=[
            pltpu.VMEM((d, dkt), BF16),
            pltpu.VMEM((ts, d), BF16),
            pltpu.VMEM((ts, dkt), BF16),
            pltpu.VMEM((ts, dkt), BF16),
            pltpu.VMEM((ts // blk, 2 * blk, dkt), BF16),
            pltpu.VMEM((dkt, ts), BF16),
            pltpu.VMEM((ts, d_inner), BF16),
            pltpu.VMEM((ts // blk, 8, dkt), F32),
            pltpu.VMEM((ts, d_inner), F32),
            pltpu.VMEM((ts // blk * GLA_HEADS, dkt // GLA_HEADS, dv), F32),
            pltpu.VMEM((ts, d_inner), F32),
            pltpu.VMEM((ts, d_inner), BF16),
            pltpu.VMEM((GLA_HEADS, dkt // GLA_HEADS, dv), F32),
        ],
        compiler_params=pltpu.CompilerParams(
            dimension_semantics=("arbitrary", "arbitrary"), vmem_limit_bytes=VMEM_LIMIT_BYTES),
        name="gla_layer_final" if final_norm else "gla_layer",
    )(x, mod, g_norm.reshape(1, d), w_qk, w_v, w_z, w_a, w_g, b_gate.reshape(1, dkt),
      g_o.reshape(1, dv), w_out.astype(BF16), g_final.reshape(1, d))


def kernel(x, c, w_ada, b_ada, g_norm, a_w_in, a_w_s, a_b_s, a_g_v, a_w_out,
           b_w_in, b_w_gate_up, b_b_gate, b_g_o, b_w_out, g_final):
    depth = w_ada.shape[0]
    assert depth % N_MIXERS == 0, "the final norm is fused into a trailing mixer-B layer"
    mod = _modulation(c, w_ada, b_ada)
    for layer in range(depth):
        j = layer // N_MIXERS
        if layer % N_MIXERS == 0:
            x = _sgu_layer(x, mod[layer], g_norm[layer], a_w_in[j], a_w_s[j], a_b_s[j], a_g_v[j], a_w_out[j])
        else:
            x = _gla_layer(x, mod[layer], g_norm[layer], b_w_in[j], b_w_gate_up[j], b_b_gate[j], b_g_o[j],
                           b_w_out[j], g_final, final_norm=(layer == depth - 1))
    return x
```

```python
import functools

import jax
import jax.numpy as jnp
from jax import lax
from jax.experimental import pallas as pl
from jax.experimental.pallas import tpu as pltpu

EPS = 1e-6
N_MIXERS = 2

SGU_CHUNK = 128
SGU_GROUPS = 8

GLA_HEADS = 4
GLA_GATE_RANK = 16
GLA_GATE_TAU = 16.0
GLA_CHUNK = 64

LANES = 128
SEQ_TILE = 512
VMEM_LIMIT_BYTES = 56 * 1024 * 1024
CAST_BLOCK_BYTES = 12 * 1024 * 1024

F32 = jnp.float32
BF16 = jnp.bfloat16


def _dot(a, b):
    return jnp.dot(a, b, preferred_element_type=F32)


def _dot_nt(a, b):
    return lax.dot_general(a, b, (((1,), (1,)), ((), ())), preferred_element_type=F32)


def _rms(x):
    return x * lax.rsqrt(jnp.mean(x * x, axis=-1, keepdims=True) + EPS)


def _modulated_norm(x, mod_ref, gn):
    shift = mod_ref[0:1, :]
    scale = mod_ref[1:2, :]
    return (_rms(x) * gn) * (1.0 + scale) + shift


def _column(row_vec):
    n = row_vec.shape[1]
    return jnp.broadcast_to(row_vec, (n, n)).T


def _const_spec(shape, index):
    return pl.BlockSpec(shape, lambda b, s: index, pipeline_mode=pl.Buffered(1))


def _cast_kernel(w_ref, o_ref, *, n_valid):
    w = w_ref[...]
    bc = w.shape[1]
    if n_valid % bc:
        col = lax.broadcasted_iota(jnp.int32, (1, bc), 1) + pl.program_id(1) * bc
        w = jnp.where(col < n_valid, w, 0.0)
    o_ref[...] = w.astype(BF16)


def _cast_bf16(w, name, block_cols=None, source_block=lambda j: j):
    n, r, c = w.shape
    if block_cols is None:
        max_cols = CAST_BLOCK_BYTES // (4 * r) // LANES * LANES
        n_blocks = pl.cdiv(c, max_cols)
        bc = pl.cdiv(pl.cdiv(c, n_blocks), LANES) * LANES
    else:
        bc, n_blocks = block_cols, c // block_cols
        assert c == bc * n_blocks
    return pl.pallas_call(
        functools.partial(_cast_kernel, n_valid=c),
        grid=(n, n_blocks),
        in_specs=[pl.BlockSpec((None, r, bc), lambda l, j: (l, 0, source_block(j)))],
        out_specs=pl.BlockSpec((None, r, bc), lambda l, j: (l, 0, j)),
        out_shape=jax.ShapeDtypeStruct((n, r, n_blocks * bc), BF16),
        compiler_params=pltpu.CompilerParams(
            dimension_semantics=("arbitrary", "arbitrary"), vmem_limit_bytes=VMEM_LIMIT_BYTES),
        name=name,
    )(w)


def _mod_kernel(c_ref, w_ref, b_ref, o_ref):
    d = c_ref.shape[1]
    cond = jax.nn.silu(c_ref[...]).astype(BF16)
    for j in range(3):
        cols = slice(j * d, (j + 1) * d)
        o_ref[:, j, :] = _dot(cond, w_ref[:, cols].astype(BF16)) + b_ref[:, cols]


def _modulation(c, w_ada, b_ada):
    depth, d, _ = w_ada.shape
    bsz = c.shape[0]
    return pl.pallas_call(
        _mod_kernel,
        grid=(depth,),
        in_specs=[
            pl.BlockSpec((bsz, d), lambda l: (0, 0)),
            pl.BlockSpec((None, d, 3 * d), lambda l: (l, 0, 0)),
            pl.BlockSpec((None, 1, 3 * d), lambda l: (l, 0, 0)),
        ],
        out_specs=pl.BlockSpec((None, bsz, 3, d), lambda l: (l, 0, 0, 0)),
        out_shape=jax.ShapeDtypeStruct((depth, bsz, 3, d), F32),
        compiler_params=pltpu.CompilerParams(
            dimension_semantics=("arbitrary",), vmem_limit_bytes=VMEM_LIMIT_BYTES),
        name="adaln_modulation",
    )(c, w_ada, b_ada.reshape(depth, 1, 3 * d))


def _sgu_layer_kernel(x_ref, mod_ref, gn_ref, wuz_ref, wv_ref, ws_ref, bs_ref, gv_ref, wout_ref,
                      o_ref, h_ref, vn_ref, y_ref, *, layer, j):
    ts = x_ref.shape[0]
    d_inner = y_ref.shape[1]
    gd = d_inner // SGU_GROUPS
    n_chunks = ts // SGU_CHUNK
    uzcols = lambda g: slice(2 * g * gd, 2 * (g + 1) * gd)
    ycols = lambda g: slice(g * gd, (g + 1) * gd)

    h_ref[...] = _modulated_norm(x_ref[...], mod_ref, gn_ref[layer:layer + 1, :]).astype(BF16)

    v = _dot(h_ref[...], wv_ref[...])
    uz_next = _dot(h_ref[...], wuz_ref[:, uzcols(0)])
    vn = (_rms(jax.nn.gelu(v)) * gv_ref[j:j + 1, :]).astype(BF16)
    for g in range(SGU_GROUPS):
        vn_ref[g] = vn[:, g * gd:(g + 1) * gd]

    row = lax.broadcasted_iota(jnp.int32, (SGU_CHUNK, SGU_CHUNK), 0)
    col = lax.broadcasted_iota(jnp.int32, (SGU_CHUNK, SGU_CHUNK), 1)
    causal = col <= row

    def mix(g):
        w = jnp.where(causal, ws_ref[g], 0.0).astype(BF16)
        bias = bs_ref[g]
        return jnp.concatenate(
            [_dot(w, vn_ref[g, c * SGU_CHUNK:(c + 1) * SGU_CHUNK, :]) + bias for c in range(n_chunks)],
            axis=0)

    mixed_next = mix(0)
    for g in range(SGU_GROUPS):
        uz, mixed = uz_next, mixed_next
        if g + 1 < SGU_GROUPS:
            uz_next = _dot(h_ref[...], wuz_ref[:, uzcols(g + 1)])
            mixed_next = mix(g + 1)
        y_ref[:, ycols(g)] = (jax.nn.gelu(uz[:, :gd]) * mixed * jax.nn.silu(uz[:, gd:])).astype(BF16)
    gate = mod_ref[2:3, :]
    o_ref[...] = x_ref[...] + gate * _dot(y_ref[...], wout_ref[...])


def _sgu_layer(x, mod, g_norm, w_in, w_s, b_s, g_v, w_out, *, layer, j):
    bsz, seq, d = x.shape
    d_inner = w_out.shape[1]
    gd = d_inner // SGU_GROUPS
    ts = SEQ_TILE
    assert w_in.shape[2] == 3 * d_inner
    x_spec = pl.BlockSpec((None, ts, d), lambda b, s: (b, s, 0))
    return pl.pallas_call(
        functools.partial(_sgu_layer_kernel, layer=layer, j=j),
        grid=(bsz, seq // ts),
        in_specs=[
            x_spec,
            pl.BlockSpec((None, None, 3, d), lambda b, s: (layer, b, 0, 0)),
            _const_spec(g_norm.shape, (0, 0)),
            _const_spec((None, d, 2 * d_inner), (j, 0, 0)),
            _const_spec((None, d, d_inner), (j, 0, 2)),
            _const_spec((None, SGU_GROUPS, SGU_CHUNK, SGU_CHUNK), (j, 0, 0, 0)),
            _const_spec((None, SGU_GROUPS, SGU_CHUNK, 1), (j, 0, 0, 0)),
            _const_spec(g_v.shape, (0, 0)),
            _const_spec((None, d_inner, d), (j, 0, 0)),
        ],
        out_specs=x_spec,
        out_shape=jax.ShapeDtypeStruct(x.shape, x.dtype),
        scratch_shapes=[
            pltpu.VMEM((ts, d), BF16),
            pltpu.VMEM((SGU_GROUPS, ts, gd), BF16),
            pltpu.VMEM((ts, d_inner), BF16),
        ],
        compiler_params=pltpu.CompilerParams(
            dimension_semantics=("arbitrary", "arbitrary"), vmem_limit_bytes=VMEM_LIMIT_BYTES),
        name="sgu_layer",
    )(x, mod, g_norm, w_in, w_in, w_s, b_s[..., None], g_v, w_out)


def _gla_layer_kernel(x_ref, mod_ref, gn_ref, win_ref, wg_ref, bg_ref, go_ref, wout_ref, gf_ref, o_ref,
                      wag_ref, h_ref, qd_ref, qp_ref, kcat_ref, kst_ref, v_ref, dec_ref, on_ref, upd_ref,
                      zs_ref, y_ref, state_ref, *, layer, j, final_norm):
    ts = x_ref.shape[0]
    dkt = qd_ref.shape[1]
    dk = dkt // GLA_HEADS
    d_inner = v_ref.shape[1]
    dv = d_inner // GLA_HEADS
    L = GLA_CHUNK
    blk = 2 * L
    n_blk = ts // blk
    qk_cols = slice(0, 2 * dkt)
    v_cols = slice(2 * dkt, 2 * dkt + d_inner)
    z0 = 2 * dkt + d_inner
    a_cols = slice(z0 + d_inner, z0 + d_inner + LANES)

    @pl.when(pl.program_id(1) == 0)
    def _():
        state_ref[...] = jnp.zeros_like(state_ref)

    @pl.when((pl.program_id(0) == 0) & (pl.program_id(1) == 0))
    def _():
        wg = jnp.concatenate([wg_ref[...].astype(BF16), jnp.zeros((LANES - GLA_GATE_RANK, dkt), BF16)], axis=0)
        wag_ref[...] = _dot(win_ref[:, a_cols], wg).astype(BF16)

    h_ref[...] = _modulated_norm(x_ref[...], mod_ref, gn_ref[layer:layer + 1, :]).astype(BF16)

    go_all = jnp.tile(go_ref[j:j + 1, :], (1, GLA_HEADS))

    def zproj(c0, c1):
        zs_ref[:, c0:c1] = jax.nn.silu(_dot(h_ref[...], win_ref[:, z0 + c0:z0 + c1])) * go_all[:, c0:c1]

    qk = _dot(h_ref[...], win_ref[:, qk_cols])
    pre = _dot(h_ref[...], wag_ref[...]) + bg_ref[j:j + 1, :]
    v_ref[...] = _dot(h_ref[...], win_ref[:, v_cols]).astype(BF16)
    log_a = jax.nn.log_sigmoid(pre) / GLA_GATE_TAU

    hi = log_a.astype(BF16)
    lo = (log_a - hi.astype(F32)).astype(BF16)
    row = lax.broadcasted_iota(jnp.int32, (blk, blk), 0)
    col = lax.broadcasted_iota(jnp.int32, (blk, blk), 1)
    causal = col <= row
    tri = jnp.where(causal & ((row >= L) == (col >= L)), 1.0, 0.0).astype(BF16)
    tri2 = jnp.concatenate([tri, tri], axis=1)
    b = jnp.concatenate(
        [_dot(tri2, jnp.concatenate([hi[i * blk:(i + 1) * blk], lo[i * blk:(i + 1) * blk]], axis=0))
         for i in range(n_blk)], axis=0)
    zproj(0, d_inner // 2)

    b4 = b.reshape(n_blk, 2, L, dkt)
    bl_lo = b4[:, 0:1, L - 1:L, :]
    bl_hi = b4[:, 1:2, L - 1:L, :]
    is_lo = lax.broadcasted_iota(jnp.int32, (1, 2, 1, 1), 1) == 0
    lo_on_hi = jnp.where(is_lo, 0.0, bl_lo)
    lo_on_lo = jnp.where(is_lo, bl_lo, 0.0)
    q4 = (qk[:, :dkt] * (dk ** -0.5)).reshape(b4.shape)
    k4 = qk[:, dkt:].reshape(b4.shape)
    flat = lambda t: t.reshape(ts, dkt)
    qd_ref[...] = flat(q4 * jnp.exp(b4)).astype(BF16)
    qp_ref[...] = flat(q4 * jnp.exp(b4 + lo_on_hi)).astype(BF16)
    k_mix = (k4 * jnp.exp(lo_on_lo - b4)).reshape(n_blk, blk, dkt)
    k_inv = (k4 * jnp.exp(-b4)).reshape(n_blk, blk, dkt)
    kcat_ref[...] = jnp.concatenate([k_mix, k_inv], axis=1).astype(BF16)
    kst_ref[...] = flat(k4 * jnp.exp(bl_hi + lo_on_lo - b4)).T.astype(BF16)
    dec_ref[...] = jnp.broadcast_to(jnp.exp(bl_lo + bl_hi).reshape(n_blk, 1, dkt), dec_ref.shape)

    blocks = [(i, hd) for i in range(n_blk) for hd in range(GLA_HEADS)]
    rows = lambda i: slice(i * blk, (i + 1) * blk)
    kcols = lambda hd: slice(hd * dk, (hd + 1) * dk)
    vcols = lambda hd: slice(hd * dv, (hd + 1) * dv)

    attn = {}
    for i, hd in blocks:
        s = _dot_nt(qd_ref[rows(i), kcols(hd)], kcat_ref[i, :, kcols(hd)])
        s = jnp.concatenate([s[:L, blk:], s[L:, :blk]], axis=0)
        attn[i, hd] = jnp.where(causal, s, 0.0).astype(BF16)
    zproj(d_inner // 2, 3 * d_inner // 4)

    for i, hd in blocks:
        lhs = jnp.concatenate([attn[i, hd], kst_ref[kcols(hd), rows(i)]], axis=0)
        ou = _dot(lhs, v_ref[rows(i), vcols(hd)])
        on_ref[rows(i), vcols(hd)] = ou[:blk]
        upd_ref[i * GLA_HEADS + hd] = ou[blk:]

    for hd in range(GLA_HEADS):
        if hd == 1:
            zproj(3 * d_inner // 4, d_inner)
        for i in range(n_blk):
            st = state_ref[hd]
            o = on_ref[rows(i), vcols(hd)] + _dot(qp_ref[rows(i), kcols(hd)], st.astype(BF16))
            decay = jnp.tile(_column(dec_ref[i, 0:1, kcols(hd)]), (1, dv // dk))
            state_ref[hd] = decay * st + upd_ref[i * GLA_HEADS + hd]
            y_ref[rows(i), vcols(hd)] = (_rms(o) * zs_ref[rows(i), vcols(hd)]).astype(BF16)

    gate = mod_ref[2:3, :]
    x_new = x_ref[...] + gate * _dot(y_ref[...], wout_ref[...])
    if final_norm:
        x_new = _rms(x_new) * gf_ref[...]
    o_ref[...] = x_new


def _gla_layer(x, mod, g_norm, w_in, w_gate_up, b_gate, g_o, w_out, g_final, *, layer, j, final_norm):
    bsz, seq, d = x.shape
    d_inner = w_out.shape[1]
    dkt = w_gate_up.shape[2]
    dk = dkt // GLA_HEADS
    dv = d_inner // GLA_HEADS
    ts = SEQ_TILE
    blk = 2 * GLA_CHUNK
    w_cols = w_in.shape[2]
    assert w_cols >= 2 * dkt + 2 * d_inner + LANES, "cast w_in must hold a lane tile for the gate path"
    x_spec = pl.BlockSpec((None, ts, d), lambda b, s: (b, s, 0))
    return pl.pallas_call(
        functools.partial(_gla_layer_kernel, layer=layer, j=j, final_norm=final_norm),
        grid=(bsz, seq // ts),
        in_specs=[
            x_spec,
            pl.BlockSpec((None, None, 3, d), lambda b, s: (layer, b, 0, 0)),
            _const_spec(g_norm.shape, (0, 0)),
            _const_spec((None, d, w_cols), (j, 0, 0)),
            _const_spec((None, GLA_GATE_RANK, dkt), (j, 0, 0)),
            _const_spec(b_gate.shape, (0, 0)),
            _const_spec(g_o.shape, (0, 0)),
            _const_spec((None, d_inner, d), (j, 0, 0)),
            _const_spec((1, d), (0, 0)),
        ],
        out_specs=x_spec,
        out_shape=jax.ShapeDtypeStruct(x.shape, x.dtype),
        scratch_shapes=[
            pltpu.VMEM((d, dkt), BF16),
            pltpu.VMEM((ts, d), BF16),
            pltpu.VMEM((ts, dkt), BF16),
            pltpu.VMEM((ts, dkt), BF16),
            pltpu.VMEM((ts // blk, 2 * blk, dkt), BF16),
            pltpu.VMEM((dkt, ts), BF16),
            pltpu.VMEM((ts, d_inner), BF16),
            pltpu.VMEM((ts // blk, 8, dkt), F32),
            pltpu.VMEM((ts, d_inner), F32),
            pltpu.VMEM((ts // blk * GLA_HEADS, dk, dv), F32),
            pltpu.VMEM((ts, d_inner), F32),
            pltpu.VMEM((ts, d_inner), BF16),
            pltpu.VMEM((GLA_HEADS, dk, dv), F32),
        ],
        compiler_params=pltpu.CompilerParams(
            dimension_semantics=("arbitrary", "arbitrary"), vmem_limit_bytes=VMEM_LIMIT_BYTES),
        name="gla_layer_final" if final_norm else "gla_layer",
    )(x, mod, g_norm, w_in, w_gate_up, b_gate, g_o, w_out, g_final.reshape(1, d))


def kernel(x, c, w_ada, b_ada, g_norm, a_w_in, a_w_s, a_b_s, a_g_v, a_w_out,
           b_w_in, b_w_gate_up, b_b_gate, b_g_o, b_w_out, g_final):
    depth = w_ada.shape[0]
    assert depth % N_MIXERS == 0, "the final norm is fused into a trailing mixer-B layer"
    mod = _modulation(c, w_ada, b_ada)
    gd = a_w_in.shape[2] // (3 * SGU_GROUPS)
    a_w_in = _cast_bf16(a_w_in, "cast_a_w_in", block_cols=gd, source_block=lambda j: jnp.where(
        j < 2 * SGU_GROUPS, j // 2 + 2 * SGU_GROUPS * (j % 2), j - SGU_GROUPS))
    a_w_out = _cast_bf16(a_w_out, "cast_a_w_out")
    b_w_in, b_w_out = _cast_bf16(b_w_in, "cast_b_w_in"), _cast_bf16(b_w_out, "cast_b_w_out")
    for layer in range(depth):
        j = layer // N_MIXERS
        if layer % N_MIXERS == 0:
            x = _sgu_layer(x, mod, g_norm, a_w_in, a_w_s, a_b_s, a_g_v, a_w_out, layer=layer, j=j)
        else:
            x = _gla_layer(x, mod, g_norm, b_w_in, b_w_gate_up, b_b_gate, b_g_o, b_w_out, g_final,
                           layer=layer, j=j, final_norm=(layer == depth - 1))
    return x
```

```python
import functools

import jax
import jax.numpy as jnp
from jax import lax
from jax.experimental import pallas as pl
from jax.experimental.pallas import tpu as pltpu

EPS = 1e-6
N_MIXERS = 2

SGU_CHUNK = 128
SGU_GROUPS = 8

GLA_HEADS = 4
GLA_GATE_RANK = 16
GLA_GATE_TAU = 16.0
GLA_CHUNK = 64

LANES = 128
SEQ_TILE = 512
VMEM_LIMIT_BYTES = 56 * 1024 * 1024
CAST_BLOCK_BYTES = 12 * 1024 * 1024

F32 = jnp.float32
BF16 = jnp.bfloat16


def _dot(a, b):
    return jnp.dot(a, b, preferred_element_type=F32)


def _dot_nt(a, b):
    return lax.dot_general(a, b, (((1,), (1,)), ((), ())), preferred_element_type=F32)


def _rms(x):
    return x * lax.rsqrt(jnp.mean(x * x, axis=-1, keepdims=True) + EPS)


def _modulated_norm(x, mod_ref, gn):
    shift = mod_ref[0:1, :]
    scale = mod_ref[1:2, :]
    return (_rms(x) * gn) * (1.0 + scale) + shift


def _column(row_vec):
    n = row_vec.shape[1]
    return jnp.broadcast_to(row_vec, (n, n)).T


def _const_spec(shape, index):
    return pl.BlockSpec(shape, lambda b, s: index, pipeline_mode=pl.Buffered(1))


def _row_spec(stacked, index):
    return _const_spec((None, 1, stacked.shape[1]), (index, 0, 0))


def _cast_kernel(w_ref, o_ref):
    o_ref[...] = w_ref[...].astype(BF16)


def _cast_bf16(w, name):
    n, r, c = w.shape
    n_blocks = pl.cdiv(4 * r * c, CAST_BLOCK_BYTES)
    bc = c // n_blocks
    assert bc * n_blocks == c and bc % LANES == 0
    return pl.pallas_call(
        _cast_kernel,
        grid=(n, n_blocks),
        in_specs=[pl.BlockSpec((None, r, bc), lambda l, j: (l, 0, j))],
        out_specs=pl.BlockSpec((None, r, bc), lambda l, j: (l, 0, j)),
        out_shape=jax.ShapeDtypeStruct(w.shape, BF16),
        compiler_params=pltpu.CompilerParams(
            dimension_semantics=("arbitrary", "arbitrary"), vmem_limit_bytes=VMEM_LIMIT_BYTES),
        name=name,
    )(w)


def _mod_kernel(c_ref, w_ref, b_ref, o_ref):
    d = c_ref.shape[1]
    cond = jax.nn.silu(c_ref[...]).astype(BF16)
    for j in range(3):
        cols = slice(j * d, (j + 1) * d)
        o_ref[:, j, :] = _dot(cond, w_ref[:, cols].astype(BF16)) + b_ref[:, cols]


def _modulation(c, w_ada, b_ada):
    depth, d, _ = w_ada.shape
    bsz = c.shape[0]
    return pl.pallas_call(
        _mod_kernel,
        grid=(depth,),
        in_specs=[
            pl.BlockSpec((bsz, d), lambda l: (0, 0)),
            pl.BlockSpec((None, d, 3 * d), lambda l: (l, 0, 0)),
            pl.BlockSpec((None, 1, 3 * d), lambda l: (l, 0, 0)),
        ],
        out_specs=pl.BlockSpec((None, bsz, 3, d), lambda l: (l, 0, 0, 0)),
        out_shape=jax.ShapeDtypeStruct((depth, bsz, 3, d), F32),
        compiler_params=pltpu.CompilerParams(
            dimension_semantics=("arbitrary",), vmem_limit_bytes=VMEM_LIMIT_BYTES),
        name="adaln_modulation",
    )(c, w_ada, b_ada.reshape(depth, 1, 3 * d))


def _sgu_layer_kernel(x_ref, mod_ref, gn_ref, win_ref, ws_ref, bs_ref, gv_ref, wout_ref,
                      o_ref, h_ref, vn_ref, y_ref):
    ts = x_ref.shape[0]
    d_inner = y_ref.shape[1]
    gd = d_inner // SGU_GROUPS
    n_chunks = ts // SGU_CHUNK
    ucols = lambda g: slice(g * gd, (g + 1) * gd)
    zcols = lambda g: slice(2 * d_inner + g * gd, 2 * d_inner + (g + 1) * gd)

    h_ref[...] = _modulated_norm(x_ref[...], mod_ref, gn_ref[...]).astype(BF16)

    v = _dot(h_ref[...], win_ref[:, d_inner:2 * d_inner])
    u_next = _dot(h_ref[...], win_ref[:, ucols(0)])
    z_next = _dot(h_ref[...], win_ref[:, zcols(0)])
    vn = (_rms(jax.nn.gelu(v)) * gv_ref[...]).astype(BF16)
    for g in range(SGU_GROUPS):
        vn_ref[g] = vn[:, g * gd:(g + 1) * gd]

    row = lax.broadcasted_iota(jnp.int32, (SGU_CHUNK, SGU_CHUNK), 0)
    col = lax.broadcasted_iota(jnp.int32, (SGU_CHUNK, SGU_CHUNK), 1)
    causal = col <= row

    def mix(g):
        w = jnp.where(causal, ws_ref[g], 0.0).astype(BF16)
        bias = bs_ref[g]
        return jnp.concatenate(
            [_dot(w, vn_ref[g, c * SGU_CHUNK:(c + 1) * SGU_CHUNK, :]) + bias for c in range(n_chunks)],
            axis=0)

    mixed_next = mix(0)
    for g in range(SGU_GROUPS):
        u, z, mixed = u_next, z_next, mixed_next
        if g + 1 < SGU_GROUPS:
            u_next = _dot(h_ref[...], win_ref[:, ucols(g + 1)])
            z_next = _dot(h_ref[...], win_ref[:, zcols(g + 1)])
            mixed_next = mix(g + 1)
        y_ref[:, ucols(g)] = (jax.nn.gelu(u) * mixed * jax.nn.silu(z)).astype(BF16)
    gate = mod_ref[2:3, :]
    o_ref[...] = x_ref[...] + gate * _dot(y_ref[...], wout_ref[...])


def _sgu_layer(x, mod, g_norm, w_in, w_s, b_s, g_v, w_out, *, layer, j):
    bsz, seq, d = x.shape
    d_inner = w_out.shape[1]
    gd = d_inner // SGU_GROUPS
    ts = SEQ_TILE
    assert w_in.shape[2] == 3 * d_inner
    x_spec = pl.BlockSpec((None, ts, d), lambda b, s: (b, s, 0))
    return pl.pallas_call(
        _sgu_layer_kernel,
        grid=(bsz, seq // ts),
        in_specs=[
            x_spec,
            pl.BlockSpec((None, None, 3, d), lambda b, s: (layer, b, 0, 0)),
            _row_spec(g_norm, layer),
            _const_spec((None, d, 3 * d_inner), (j, 0, 0)),
            _const_spec((None, SGU_GROUPS, SGU_CHUNK, SGU_CHUNK), (j, 0, 0, 0)),
            _const_spec((None, SGU_GROUPS, SGU_CHUNK, 1), (j, 0, 0, 0)),
            _row_spec(g_v, j),
            _const_spec((None, d_inner, d), (j, 0, 0)),
        ],
        out_specs=x_spec,
        out_shape=jax.ShapeDtypeStruct(x.shape, x.dtype),
        scratch_shapes=[
            pltpu.VMEM((ts, d), BF16),
            pltpu.VMEM((SGU_GROUPS, ts, gd), BF16),
            pltpu.VMEM((ts, d_inner), BF16),
        ],
        compiler_params=pltpu.CompilerParams(
            dimension_semantics=("arbitrary", "arbitrary"), vmem_limit_bytes=VMEM_LIMIT_BYTES),
        name="sgu_layer",
    )(x, mod, g_norm[:, None, :], w_in, w_s, b_s[..., None], g_v[:, None, :], w_out)


def _gla_layer_kernel(x_ref, mod_ref, gn_ref, win_ref, wg_ref, bg_ref, go_ref, wout_ref, gf_ref, o_ref,
                      wag_ref, h_ref, qd_ref, qp_ref, kcat_ref, kst_ref, v_ref, dec_ref, on_ref, upd_ref,
                      zs_ref, y_ref, state_ref, *, final_norm):
    ts = x_ref.shape[0]
    dkt = qd_ref.shape[1]
    dk = dkt // GLA_HEADS
    d_inner = v_ref.shape[1]
    dv = d_inner // GLA_HEADS
    L = GLA_CHUNK
    blk = 2 * L
    n_blk = ts // blk
    qk_cols = slice(0, 2 * dkt)
    v_cols = slice(2 * dkt, 2 * dkt + d_inner)
    z0 = 2 * dkt + d_inner
    a_cols = slice(z0 + d_inner, z0 + d_inner + LANES)

    @pl.when(pl.program_id(1) == 0)
    def _():
        state_ref[...] = jnp.zeros_like(state_ref)

    @pl.when((pl.program_id(0) == 0) & (pl.program_id(1) == 0))
    def _():
        wg = jnp.concatenate([wg_ref[...].astype(BF16), jnp.zeros((LANES - GLA_GATE_RANK, dkt), BF16)], axis=0)
        wag_ref[...] = _dot(win_ref[:, a_cols], wg).astype(BF16)

    h_ref[...] = _modulated_norm(x_ref[...], mod_ref, gn_ref[...]).astype(BF16)

    go_all = jnp.tile(go_ref[...], (1, GLA_HEADS))

    def zproj(c0, c1):
        zs_ref[:, c0:c1] = jax.nn.silu(_dot(h_ref[...], win_ref[:, z0 + c0:z0 + c1])) * go_all[:, c0:c1]

    qk = _dot(h_ref[...], win_ref[:, qk_cols])
    pre = _dot(h_ref[...], wag_ref[...]) + bg_ref[...]
    v_ref[...] = _dot(h_ref[...], win_ref[:, v_cols]).astype(BF16)
    log_a = jax.nn.log_sigmoid(pre) / GLA_GATE_TAU

    hi = log_a.astype(BF16)
    lo = (log_a - hi.astype(F32)).astype(BF16)
    row = lax.broadcasted_iota(jnp.int32, (blk, blk), 0)
    col = lax.broadcasted_iota(jnp.int32, (blk, blk), 1)
    causal = col <= row
    tri = jnp.where(causal & ((row >= L) == (col >= L)), 1.0, 0.0).astype(BF16)
    tri2 = jnp.concatenate([tri, tri], axis=1)
    b = jnp.concatenate(
        [_dot(tri2, jnp.concatenate([hi[i * blk:(i + 1) * blk], lo[i * blk:(i + 1) * blk]], axis=0))
         for i in range(n_blk)], axis=0)
    zproj(0, d_inner // 2)

    b4 = b.reshape(n_blk, 2, L, dkt)
    bl_lo = b4[:, 0:1, L - 1:L, :]
    bl_hi = b4[:, 1:2, L - 1:L, :]
    is_lo = lax.broadcasted_iota(jnp.int32, (1, 2, 1, 1), 1) == 0
    lo_on_hi = jnp.where(is_lo, 0.0, bl_lo)
    lo_on_lo = jnp.where(is_lo, bl_lo, 0.0)
    q4 = (qk[:, :dkt] * (dk ** -0.5)).reshape(b4.shape)
    k4 = qk[:, dkt:].reshape(b4.shape)
    flat = lambda t: t.reshape(ts, dkt)
    qd_ref[...] = flat(q4 * jnp.exp(b4)).astype(BF16)
    qp_ref[...] = flat(q4 * jnp.exp(b4 + lo_on_hi)).astype(BF16)
    k_mix = (k4 * jnp.exp(lo_on_lo - b4)).reshape(n_blk, blk, dkt)
    k_inv = (k4 * jnp.exp(-b4)).reshape(n_blk, blk, dkt)
    kcat_ref[...] = jnp.concatenate([k_mix, k_inv], axis=1).astype(BF16)
    kst_ref[...] = flat(k4 * jnp.exp(bl_hi + lo_on_lo - b4)).T.astype(BF16)
    dec_ref[...] = jnp.broadcast_to(jnp.exp(bl_lo + bl_hi).reshape(n_blk, 1, dkt), dec_ref.shape)

    blocks = [(i, hd) for i in range(n_blk) for hd in range(GLA_HEADS)]
    rows = lambda i: slice(i * blk, (i + 1) * blk)
    kcols = lambda hd: slice(hd * dk, (hd + 1) * dk)
    vcols = lambda hd: slice(hd * dv, (hd + 1) * dv)

    attn = {}
    for i, hd in blocks:
        s = _dot_nt(qd_ref[rows(i), kcols(hd)], kcat_ref[i, :, kcols(hd)])
        s = jnp.concatenate([s[:L, blk:], s[L:, :blk]], axis=0)
        attn[i, hd] = jnp.where(causal, s, 0.0).astype(BF16)
    zproj(d_inner // 2, 3 * d_inner // 4)

    for i, hd in blocks:
        lhs = jnp.concatenate([attn[i, hd], kst_ref[kcols(hd), rows(i)]], axis=0)
        ou = _dot(lhs, v_ref[rows(i), vcols(hd)])
        on_ref[rows(i), vcols(hd)] = ou[:blk]
        upd_ref[i * GLA_HEADS + hd] = ou[blk:]

    for hd in range(GLA_HEADS):
        if hd == 1:
            zproj(3 * d_inner // 4, d_inner)
        for i in range(n_blk):
            st = state_ref[hd]
            o = on_ref[rows(i), vcols(hd)] + _dot(qp_ref[rows(i), kcols(hd)], st.astype(BF16))
            decay = jnp.tile(_column(dec_ref[i, 0:1, kcols(hd)]), (1, dv // dk))
            state_ref[hd] = decay * st + upd_ref[i * GLA_HEADS + hd]
            y_ref[rows(i), vcols(hd)] = (_rms(o) * zs_ref[rows(i), vcols(hd)]).astype(BF16)

    gate = mod_ref[2:3, :]
    x_new = x_ref[...] + gate * _dot(y_ref[...], wout_ref[...])
    if final_norm:
        x_new = _rms(x_new) * gf_ref[...]
    o_ref[...] = x_new


def _gla_layer(x, mod, g_norm, w_in, w_gate_up, b_gate, g_o, w_out, g_final, *, layer, j, final_norm):
    bsz, seq, d = x.shape
    d_inner = w_out.shape[1]
    dkt = w_gate_up.shape[2]
    dk = dkt // GLA_HEADS
    dv = d_inner // GLA_HEADS
    ts = SEQ_TILE
    blk = 2 * GLA_CHUNK
    w_cols = w_in.shape[2]
    assert w_cols >= 2 * dkt + 2 * d_inner + LANES, "cast w_in must hold a lane tile for the gate path"
    x_spec = pl.BlockSpec((None, ts, d), lambda b, s: (b, s, 0))
    return pl.pallas_call(
        functools.partial(_gla_layer_kernel, final_norm=final_norm),
        grid=(bsz, seq // ts),
        in_specs=[
            x_spec,
            pl.BlockSpec((None, None, 3, d), lambda b, s: (layer, b, 0, 0)),
            _row_spec(g_norm, layer),
            _const_spec((None, d, w_cols), (j, 0, 0)),
            _const_spec((None, GLA_GATE_RANK, dkt), (j, 0, 0)),
            _row_spec(b_gate, j),
            _row_spec(g_o, j),
            _const_spec((None, d_inner, d), (j, 0, 0)),
            _const_spec((1, d), (0, 0)),
        ],
        out_specs=x_spec,
        out_shape=jax.ShapeDtypeStruct(x.shape, x.dtype),
        scratch_shapes=[
            pltpu.VMEM((d, dkt), BF16),
            pltpu.VMEM((ts, d), BF16),
            pltpu.VMEM((ts, dkt), BF16),
            pltpu.VMEM((ts, dkt), BF16),
            pltpu.VMEM((ts // blk, 2 * blk, dkt), BF16),
            pltpu.VMEM((dkt, ts), BF16),
            pltpu.VMEM((ts, d_inner), BF16),
            pltpu.VMEM((ts // blk, 8, dkt), F32),
            pltpu.VMEM((ts, d_inner), F32),
            pltpu.VMEM((ts // blk * GLA_HEADS, dk, dv), F32),
            pltpu.VMEM((ts, d_inner), F32),
            pltpu.VMEM((ts, d_inner), BF16),
            pltpu.VMEM((GLA_HEADS, dk, dv), F32),
        ],
        compiler_params=pltpu.CompilerParams(
            dimension_semantics=("arbitrary", "arbitrary"), vmem_limit_bytes=VMEM_LIMIT_BYTES),
        name="gla_layer_final" if final_norm else "gla_layer",
    )(x, mod, g_norm[:, None, :], w_in, w_gate_up, b_gate[:, None, :], g_o[:, None, :], w_out, g_final.reshape(1, d))


def kernel(x, c, w_ada, b_ada, g_norm, a_w_in, a_w_s, a_b_s, a_g_v, a_w_out,
           b_w_in, b_w_gate_up, b_b_gate, b_g_o, b_w_out, g_final):
    depth = w_ada.shape[0]
    assert depth % N_MIXERS == 0, "the final norm is fused into a trailing mixer-B layer"
    mod = _modulation(c, w_ada, b_ada)
    a_w_in, a_w_out = _cast_bf16(a_w_in, "cast_a_w_in"), _cast_bf16(a_w_out, "cast_a_w_out")
    b_w_in = jnp.pad(b_w_in.astype(BF16), ((0, 0), (0, 0), (0, LANES - GLA_GATE_RANK)))
    b_w_out = _cast_bf16(b_w_out, "cast_b_w_out")
    for layer in range(depth):
        j = layer // N_MIXERS
        if layer % N_MIXERS == 0:
            x = _sgu_layer(x, mod, g_norm, a_w_in, a_w_s, a_b_s, a_g_v, a_w_out, layer=layer, j=j)
        else:
            x = _gla_layer(x, mod, g_norm, b_w_in, b_w_gate_up, b_b_gate, b_g_o, b_w_out, g_final,
                           layer=layer, j=j, final_norm=(layer == depth - 1))
    return x
```
